```python
import jax, jax.numpy as jnp
from jax import lax
import numpy as np

D_MODEL = 4096
BATCH = 4
SEQ = 2048
DEPTH = 2
DEC_BATCH = 1
DEC_SEQ = 8192
PAST_LEN = 128

HEAD_DIM = 64
ATT_WIDTH = D_MODEL // 2
RWKV_WIDTH = D_MODEL - ATT_WIDTH
N_ATT_HEADS = ATT_WIDTH // HEAD_DIM
N_RWKV_HEADS = RWKV_WIDTH // HEAD_DIM
D_FF = 4 * D_MODEL
ROT_DIM = HEAD_DIM // 4
ROPE_THETA = 500000.0
DILATED_PATTERNS = ((128, 1), (512, 4), (2048, 16))
DECAY_LORA = max(32, int(round(1.8 * D_MODEL ** 0.5 / 32)) * 32)
AAA_LORA = max(32, int(round(1.8 * D_MODEL ** 0.5 / 32)) * 32)
MV_LORA = max(32, int(round(1.3 * D_MODEL ** 0.5 / 32)) * 32)
GATE_LORA = max(32, int(round(0.6 * D_MODEL ** 0.8 / 32)) * 32)
LN_EPS = 1e-5
GN_EPS = 64e-5
ALPHA = (2 * DEPTH) ** 0.25
BETA = (8 * DEPTH) ** -0.25
NEG_INF = -1e30

ATT_SPLITS = (ATT_WIDTH, ATT_WIDTH, ATT_WIDTH)
RW_SPLITS = (RWKV_WIDTH, RWKV_WIDTH, RWKV_WIDTH,
             DECAY_LORA, DECAY_LORA, AAA_LORA, AAA_LORA, GATE_LORA)
ATT_COLS = sum(ATT_SPLITS)
RW_COLS = sum(RW_SPLITS)
IN_COLS = ATT_COLS + RW_COLS

kernel_name = "hybrid_dilated_attn_rwkv7_encoder"


def layer_norm(x, g, b):
    xf = x.astype(jnp.float32)
    mu = xf.mean(-1, keepdims=True)
    var = jnp.square(xf - mu).mean(-1, keepdims=True)
    return ((xf - mu) * lax.rsqrt(var + LN_EPS) * g + b).astype(x.dtype)


def partial_rotary(x):
    S = x.shape[1]
    half = ROT_DIM // 2
    inv_freq = jnp.power(ROPE_THETA, -jnp.arange(half, dtype=jnp.float32) * 2.0 / ROT_DIM)
    ang = jnp.arange(S, dtype=jnp.float32)[:, None] * inv_freq[None, :]
    cos = jnp.cos(ang)[None, :, None, :]
    sin = jnp.sin(ang)[None, :, None, :]
    xf = x.astype(jnp.float32)
    x1, x2, rest = xf[..., :half], xf[..., half:ROT_DIM], xf[..., ROT_DIM:]
    return jnp.concatenate([x1 * cos - x2 * sin, x2 * cos + x1 * sin, rest], axis=-1)


def banded_window_attention(q, k, v, half):
    N, L, H, Dh = q.shape
    nb = -(-L // half)
    Lp = nb * half
    qb = jnp.pad(q, ((0, 0), (0, Lp - L), (0, 0), (0, 0))).reshape(N, nb, half, H, Dh)
    kv_pad = ((0, 0), (half, Lp - L + half), (0, 0), (0, 0))

    def band(t):
        tb = jnp.pad(t, kv_pad).reshape(N, nb + 2, half, H, Dh)
        return jnp.concatenate([tb[:, :-2], tb[:, 1:-1], tb[:, 2:]], axis=2)

    kb, vb = band(k), band(v)
    qpos = jnp.arange(Lp).reshape(nb, half)
    kpos = (jnp.arange(nb)[:, None] - 1) * half + jnp.arange(3 * half)[None, :]
    rel = kpos[:, None, :] - qpos[:, :, None]
    mask = (jnp.abs(rel) <= half) & (kpos[:, None, :] >= 0) & (kpos[:, None, :] < L)
    s = jnp.einsum('nbqhd,nbkhd->nbhqk', qb, kb)
    s = jnp.where(mask[None, :, None], s, NEG_INF)
    m = s.max(-1, keepdims=True)
    p = jnp.exp(s - m)
    denom = p.sum(-1, keepdims=True)
    o = jnp.einsum('nbhqk,nbkhd->nbqhd', p / denom, vb).reshape(N, Lp, H, Dh)[:, :L]
    lse = (m + jnp.log(denom))[..., 0]
    lse = lse.transpose(0, 1, 3, 2).reshape(N, Lp, H)[:, :L]
    return o, lse


def dilated_attention(q, k, v, window, dilation):
    B, S, H, Dh = q.shape
    L = S // dilation

    def to_sub(t):
        return t.reshape(B, L, dilation, H, Dh).transpose(0, 2, 1, 3, 4).reshape(B * dilation, L, H, Dh)

    o, lse = banded_window_attention(to_sub(q), to_sub(k), to_sub(v), window // (2 * dilation))
    o = o.reshape(B, dilation, L, H, Dh).transpose(0, 2, 1, 3, 4).reshape(B, S, H, Dh)
    lse = lse.reshape(B, dilation, L, H).transpose(0, 2, 1, 3).reshape(B, S, H)
    return o, lse


def centred_token_shift(z, mu_prev, mu_next):
    zp = jnp.pad(z, ((0, 0), (1, 0), (0, 0)))[:, :-1]
    zn = jnp.pad(z, ((0, 0), (0, 1), (0, 0)))[:, 1:]
    return z + mu_prev * (zp - z) + mu_next * (zn - z)


def wkv7_scan(r, decay, k, v, aa, bb, reverse):
    B, S, H, N = r.shape

    def step(state, inp):
        r_t, w_t, k_t, v_t, a_t, b_t = inp
        sa = jnp.einsum('bhij,bhj->bhi', state, a_t)
        state = (state * w_t[:, :, None, :] + sa[..., None] * b_t[:, :, None, :]
                 + v_t[..., None] * k_t[:, :, None, :])
        return state, jnp.einsum('bhij,bhj->bhi', state, r_t)

    xs = tuple(jnp.swapaxes(t.astype(jnp.float32), 0, 1) for t in (r, decay, k, v, aa, bb))
    s0 = jnp.zeros((B, H, N, N), jnp.float32)
    _, y = lax.scan(step, s0, xs, reverse=reverse)
    return jnp.swapaxes(y, 0, 1)


def encoder_layer(x, v_first, l, P):
    B, S, _ = x.shape
    f32 = jnp.float32
    w = P['w_in'][l]
    if l > 0:
        w = jnp.concatenate([w, P['w_vres_in'][l - 1]], axis=1)
    z = x @ w

    qa, ka, va = jnp.split(z[..., :ATT_COLS], 3, axis=-1)
    ah = lambda t: t.reshape(B, S, N_ATT_HEADS, HEAD_DIM)
    qa = partial_rotary(ah(qa)) * (HEAD_DIM ** -0.5)
    ka = partial_rotary(ah(ka))
    va = ah(va).astype(f32)
    outs, lses = [], []
    for window, dilation in DILATED_PATTERNS:
        o_g, lse_g = dilated_attention(qa, ka, va, window, dilation)
        outs.append(o_g)
        lses.append(lse_g)
    wts = jax.nn.softmax(jnp.stack(lses, 0), axis=0)
    att_out = jnp.einsum('gbsh,gbshd->bshd', wts, jnp.stack(outs, 0)).reshape(B, S, ATT_WIDTH)

    zr = centred_token_shift(z[..., ATT_COLS:IN_COLS].astype(f32), P['mu_prev'][l], P['mu_next'][l])
    cuts = np.cumsum(RW_SPLITS)[:-1].tolist()
    r, kr, vr, wdf, wdb, adf, adb, gd = jnp.split(zr, cuts, axis=-1)
    rh = lambda t: t.reshape(B, S, N_RWKV_HEADS, HEAD_DIM)
    r_h = rh(r)
    k_h = rh(kr)
    kk = rh(kr * P['k_k'][l])
    kk = kk / jnp.maximum(jnp.linalg.norm(kk, axis=-1, keepdims=True), 1e-12)
    v_h = rh(vr)
    if l == 0:
        v_first = v_h
    else:
        z_vres = z[..., IN_COLS:].astype(f32)
        v_gate = jax.nn.sigmoid(P['v0'][l - 1] + z_vres @ P['v_lora_up'][l - 1])
        v_h = v_h + (v_first - v_h) * rh(v_gate)
    k_a = P['k_a'][l].reshape(N_RWKV_HEADS, HEAD_DIM)
    r_k = P['r_k'][l].reshape(N_RWKV_HEADS, HEAD_DIM)
    y = 0.0
    bonus = 0.0
    for d, wd, ad in ((0, wdf, adf), (1, wdb, adb)):
        w_log = -jax.nn.softplus(-(P['w0'][l, d] + jnp.tanh(wd) @ P['w_lora_up'][l, d])) - 0.5
        decay = rh(jnp.exp(-jnp.exp(w_log)))
        a = rh(jax.nn.sigmoid(P['a0'][l, d] + ad @ P['a_lora_up'][l, d]))
        k_d = k_h * (1.0 + (a - 1.0) * k_a)
        y = y + wkv7_scan(r_h, decay, k_d, v_h, -kk, kk * a, reverse=(d == 1))
        bonus = bonus + (r_h * k_d * r_k).sum(-1, keepdims=True) * v_h
    mu = y.mean(-1, keepdims=True)
    var = jnp.square(y - mu).mean(-1, keepdims=True)
    yn = ((y - mu) * lax.rsqrt(var + GN_EPS) * P['lnx_g'][l].reshape(N_RWKV_HEADS, HEAD_DIM)
          + P['lnx_b'][l].reshape(N_RWKV_HEADS, HEAD_DIM))
    g = jax.nn.sigmoid(gd) @ P['g_lora_up'][l]
    rw_out = (yn + bonus).reshape(B, S, RWKV_WIDTH) * g

    mix = jnp.concatenate([att_out, rw_out], axis=-1).astype(x.dtype) @ P['w_out'][l]
    x = layer_norm(ALPHA * x + mix, P['ln1_g'][l], P['ln1_b'][l])

    h = jnp.square(jax.nn.relu(x @ P['w_ff1'][l])) @ P['w_ff2'][l]
    x = layer_norm(ALPHA * x + h, P['ln2_g'][l], P['ln2_b'][l])
    return x, v_first


def run_trunk(x, P):
    v_first = None
    for l in range(DEPTH):
        x, v_first = encoder_layer(x, v_first, l, P)
    return x


def setup_inputs(seed: int = 0) -> dict:
    key = jax.random.key(seed)
    ks = jax.random.split(key, 32)
    nrm = lambda k, shape, scale: jax.random.normal(k, shape, jnp.float32) * scale
    L = DEPTH
    return {
        'x_prompt': nrm(ks[0], (BATCH, SEQ, D_MODEL), 1.0),
        'x_sample': nrm(ks[1], (DEC_BATCH, DEC_SEQ, D_MODEL), 1.0),
        'w_in': nrm(ks[2], (L, D_MODEL, IN_COLS), D_MODEL ** -0.5),
        'w_vres_in': nrm(ks[3], (L - 1, D_MODEL, MV_LORA), D_MODEL ** -0.5),
        'mu_prev': jax.random.uniform(ks[4], (L, RW_COLS), jnp.float32, 0.0, 0.5),
        'mu_next': jax.random.uniform(ks[5], (L, RW_COLS), jnp.float32, 0.0, 0.5),
        'w0': jax.random.uniform(ks[6], (L, 2, RWKV_WIDTH), jnp.float32, -6.0, -1.0),
        'w_lora_up': nrm(ks[7], (L, 2, DECAY_LORA, RWKV_WIDTH), 0.1 * DECAY_LORA ** -0.5),
        'a0': nrm(ks[8], (L, 2, RWKV_WIDTH), 0.1),
        'a_lora_up': nrm(ks[9], (L, 2, AAA_LORA, RWKV_WIDTH), 0.1 * AAA_LORA ** -0.5),
        'v0': 0.5 + nrm(ks[10], (L - 1, RWKV_WIDTH), 0.1),
        'v_lora_up': nrm(ks[11], (L - 1, MV_LORA, RWKV_WIDTH), 0.1 * MV_LORA ** -0.5),
        'g_lora_up': nrm(ks[12], (L, GATE_LORA, RWKV_WIDTH), GATE_LORA ** -0.5),
        'k_k': 0.85 + nrm(ks[13], (L, RWKV_WIDTH), 0.05),
        'k_a': 1.0 + nrm(ks[14], (L, RWKV_WIDTH), 0.05),
        'r_k': nrm(ks[15], (L, RWKV_WIDTH), 0.1),
        'lnx_g': 1.0 + nrm(ks[16], (L, RWKV_WIDTH), 0.02),
        'lnx_b': nrm(ks[17], (L, RWKV_WIDTH), 0.02),
        'w_out': nrm(ks[18], (L, D_MODEL, D_MODEL), BETA * D_MODEL ** -0.5),
        'ln1_g': 1.0 + nrm(ks[19], (L, D_MODEL), 0.02),
        'ln1_b': nrm(ks[20], (L, D_MODEL), 0.02),
        'w_ff1': nrm(ks[21], (L, D_MODEL, D_FF), D_MODEL ** -0.5),
        'w_ff2': nrm(ks[22], (L, D_FF, D_MODEL), BETA * D_FF ** -0.5),
        'ln2_g': 1.0 + nrm(ks[23], (L, D_MODEL), 0.02),
        'ln2_b': nrm(ks[24], (L, D_MODEL), 0.02),
    }


def reference(x_prompt, x_sample, w_in, w_vres_in, mu_prev, mu_next, w0, w_lora_up, a0,
              a_lora_up, v0, v_lora_up, g_lora_up, k_k, k_a, r_k, lnx_g, lnx_b, w_out,
              ln1_g, ln1_b, w_ff1, w_ff2, ln2_g, ln2_b):
    P = dict(w_in=w_in, w_vres_in=w_vres_in, mu_prev=mu_prev, mu_next=mu_next, w0=w0,
             w_lora_up=w_lora_up, a0=a0, a_lora_up=a_lora_up, v0=v0, v_lora_up=v_lora_up,
             g_lora_up=g_lora_up, k_k=k_k, k_a=k_a, r_k=r_k, lnx_g=lnx_g, lnx_b=lnx_b,
             w_out=w_out, ln1_g=ln1_g, ln1_b=ln1_b, w_ff1=w_ff1, w_ff2=w_ff2,
             ln2_g=ln2_g, ln2_b=ln2_b)
    y_prompt = run_trunk(x_prompt, P)
    y_sample = run_trunk(x_sample, P)
    return (y_prompt, y_sample)
```

```python
import dataclasses
import functools
import math

import jax
import jax.numpy as jnp
import numpy as np
from jax import lax
from jax.experimental import pallas as pl
from jax.experimental.pallas import tpu as pltpu

F32 = jnp.float32
BF16 = jnp.bfloat16

LANES = 128
HEAD_DIM = 64
HEADS_PER_GROUP = 4
GROUP_LANES = HEADS_PER_GROUP * HEAD_DIM
CHUNK = 64
SCAN_BLOCK = 256
ATT_BLOCK = 256
ROT_DIM = HEAD_DIM // 4
ROPE_THETA = 500000.0
DILATED_PATTERNS = ((128, 1), (512, 4), (2048, 16))
LN_EPS = 1e-5
GN_EPS = 64e-5
NEG_INF = -1e30
VMEM_LIMIT = 60 * 1024 * 1024


def _round_up(n, m):
    return (n + m - 1) // m * m


@dataclasses.dataclass(frozen=True)
class Dims:
    d_model: int
    d_ff: int
    depth: int
    seqs: tuple
    decay_lora: int
    aaa_lora: int
    mv_lora: int
    gate_lora: int

    @property
    def att_width(self):
        return self.d_model // 2

    @property
    def rw_width(self):
        return self.d_model - self.att_width

    @property
    def tokens(self):
        return sum(b * s for b, s in self.seqs)

    @property
    def seq_ranges(self):
        out, t = [], 0
        for b, s in self.seqs:
            for _ in range(b):
                out.append((t, t + s))
                t += s
        return tuple(out)

    @property
    def alpha(self):
        return (2 * self.depth) ** 0.25

    @property
    def rw_layout(self):
        w = self.rw_width
        names = (("r", w), ("k", w), ("v", w), ("wd0", self.decay_lora), ("wd1", self.decay_lora),
                 ("ad0", self.aaa_lora), ("ad1", self.aaa_lora), ("gd", self.gate_lora))
        lay, src, dst = {}, 0, 0
        for name, n in names:
            lay[name] = (src, dst, n, _round_up(n, LANES))
            src += n
            dst += _round_up(n, LANES)
        return lay, src, dst


def _positions(dims):
    pos = np.zeros((dims.tokens,), np.int64)
    for lo, hi in dims.seq_ranges:
        pos[lo:hi] = np.arange(hi - lo)
    return pos


def _seg_bounds(i, ranges, unit):
    lo = jnp.int32(0)
    hi = jnp.int32(0)
    for a, b in ranges:
        inside = jnp.logical_and(i >= a // unit, i < b // unit)
        lo = jnp.where(inside, a // unit, lo)
        hi = jnp.where(inside, b // unit, hi)
    return lo, hi


def _is_any(i, values):
    out = i == values[0]
    for v in values[1:]:
        out = jnp.logical_or(out, i == v)
    return out


def _cast_x_once(x_ref, xb_ref):
    @pl.when(pl.program_id(1) == 0)
    def _():
        xb_ref[...] = x_ref[...].astype(BF16)


def _proj_plain_kernel(x_ref, w_ref, o_ref, xb_ref):
    _cast_x_once(x_ref, xb_ref)
    o_ref[...] = jnp.dot(xb_ref[...], w_ref[...], preferred_element_type=F32)


def _proj_plain(x, w, *, tm, tn):
    t, d = x.shape
    n = w.shape[1]
    return pl.pallas_call(
        _proj_plain_kernel,
        grid=(t // tm, n // tn),
        in_specs=[pl.BlockSpec((tm, d), lambda i, j: (i, 0)),
                  pl.BlockSpec((d, tn), lambda i, j: (0, j))],
        out_specs=pl.BlockSpec((tm, tn), lambda i, j: (i, j)),
        out_shape=jax.ShapeDtypeStruct((t, n), F32),
        scratch_shapes=[pltpu.VMEM((tm, d), BF16)],
        compiler_params=pltpu.CompilerParams(
            dimension_semantics=("arbitrary", "arbitrary"), vmem_limit_bytes=VMEM_LIMIT),
        name="proj_plain",
    )(x, w)


def _proj_qkv_kernel(x_ref, w_ref, c_ref, s1_ref, s2_ref, o_ref, xb_ref, *, tn, width):
    _cast_x_once(x_ref, xb_ref)
    j = pl.program_id(1)
    acc = jnp.dot(xb_ref[...], w_ref[...], preferred_element_type=F32)
    tiles_per_part = width // tn

    def rotated(scale):
        c, s1, s2 = c_ref[...], s1_ref[...], s2_ref[...]
        for g in range(tn // LANES):
            a = acc[:, g * LANES:(g + 1) * LANES]
            rot = (a * c + pltpu.roll(a, ROT_DIM // 2, 1) * s1
                   + pltpu.roll(a, LANES - ROT_DIM // 2, 1) * s2)
            if scale != 1.0:
                rot = rot * scale
            o_ref[:, g * LANES:(g + 1) * LANES] = rot.astype(o_ref.dtype)

    @pl.when(j < tiles_per_part)
    def _():
        rotated(HEAD_DIM ** -0.5)

    @pl.when(jnp.logical_and(j >= tiles_per_part, j < 2 * tiles_per_part))
    def _():
        rotated(1.0)

    @pl.when(j >= 2 * tiles_per_part)
    def _():
        o_ref[...] = acc.astype(o_ref.dtype)


def _proj_qkv(x, w, rot_c, rot_s1, rot_s2, *, tm, tn):
    t, d = x.shape
    n = w.shape[1]
    width = n // 3
    tab = pl.BlockSpec((tm, LANES), lambda i, j: (i, 0))
    return pl.pallas_call(
        functools.partial(_proj_qkv_kernel, tn=tn, width=width),
        grid=(t // tm, n // tn),
        in_specs=[pl.BlockSpec((tm, d), lambda i, j: (i, 0)),
                  pl.BlockSpec((d, tn), lambda i, j: (0, j)), tab, tab, tab],
        out_specs=pl.BlockSpec((tm, tn), lambda i, j: (i, j)),
        out_shape=jax.ShapeDtypeStruct((t, n), BF16),
        scratch_shapes=[pltpu.VMEM((tm, d), BF16)],
        compiler_params=pltpu.CompilerParams(
            dimension_semantics=("arbitrary", "arbitrary"), vmem_limit_bytes=VMEM_LIMIT),
        name="proj_qkv",
    )(x, w, rot_c, rot_s1, rot_s2)


def _rotary_tables(dims):
    half = ROT_DIM // 2
    pos = jnp.asarray(_positions(dims), F32)
    inv_freq = jnp.power(ROPE_THETA, -jnp.arange(half, dtype=F32) * 2.0 / ROT_DIM)
    ang = pos[:, None] * inv_freq[None, :]
    cos, sin = jnp.cos(ang), jnp.sin(ang)
    t = dims.tokens
    ones = jnp.ones((t, HEAD_DIM - ROT_DIM), F32)
    zeros_rest = jnp.zeros((t, HEAD_DIM - ROT_DIM), F32)
    zeros_half = jnp.zeros((t, half), F32)
    c = jnp.concatenate([cos, cos, ones], axis=1)
    s1 = jnp.concatenate([zeros_half, sin, zeros_rest], axis=1)
    s2 = jnp.concatenate([-sin, zeros_half, zeros_rest], axis=1)
    rep = LANES // HEAD_DIM
    return tuple(jnp.tile(a, (1, rep)) for a in (c, s1, s2))


def _attention_bias(blk):
    reach = max(w // 2 for w, _ in DILATED_PATTERNS)
    nwin = -(-reach // blk)
    qi = np.arange(blk)[:, None]
    ki = np.arange(blk)[None, :]
    tabs = []
    for j in range(-nwin, nwin + 1):
        d = j * blk + ki - qi
        mult = np.zeros(d.shape, np.int64)
        for window, dil in DILATED_PATTERNS:
            half = window // (2 * dil)
            mult += ((d % dil == 0) & (np.abs(d) <= half * dil)).astype(np.int64)
        with np.errstate(divide="ignore"):
            tabs.append(np.where(mult > 0, np.log(np.maximum(mult, 1)), NEG_INF))
    return np.stack(tabs).astype(np.float32), nwin


def _attention_kernel(q_ref, k_ref, v_ref, bias_ref, o_ref, m_ref, l_ref, acc_ref, *,
                      blk, nwin, ranges):
    qi = pl.program_id(1)
    seg_lo, seg_hi = _seg_bounds(qi, ranges, blk)
    lo = jnp.maximum(qi - nwin, seg_lo)
    hi = jnp.minimum(qi + nwin, seg_hi - 1)

    q = q_ref[...]
    lane = lax.broadcasted_iota(jnp.int32, q.shape, 1)
    first = lane < HEAD_DIM
    zero = jnp.zeros_like(q)
    qs = jnp.concatenate([jnp.where(first, q, zero), jnp.where(first, zero, q)], axis=0)

    def scores(kb):
        start = pl.multiple_of(kb * blk, blk)
        kblk = k_ref[pl.ds(start, blk), :]
        s = lax.dot_general(qs, kblk, (((1,), (1,)), ((), ())), preferred_element_type=F32)
        b = bias_ref[kb - qi + nwin]
        return s + jnp.concatenate([b, b], axis=0), start

    s, start = scores(qi)
    m0 = jnp.max(s, axis=1, keepdims=True)
    p = jnp.exp(s - m0)
    m_ref[...] = jnp.broadcast_to(m0, m_ref.shape)
    l_ref[...] = jnp.broadcast_to(jnp.sum(p, axis=1, keepdims=True), l_ref.shape)
    acc_ref[...] = jnp.dot(p.astype(BF16), v_ref[pl.ds(start, blk), :], preferred_element_type=F32)

    def body(kb, carry):
        s, start = scores(kb)
        m_prev = m_ref[...]
        m_next = jnp.maximum(m_prev, jnp.max(s, axis=1, keepdims=True))
        alpha = jnp.exp(m_prev - m_next)
        p = jnp.exp(s - jnp.tile(m_next, (1, blk // LANES)))
        l_ref[...] = alpha * l_ref[...] + jnp.sum(p, axis=1, keepdims=True)
        acc_ref[...] = alpha * acc_ref[...] + jnp.dot(
            p.astype(BF16), v_ref[pl.ds(start, blk), :], preferred_element_type=F32)
        m_ref[...] = m_next
        return carry

    lax.fori_loop(lo, qi, body, 0)
    lax.fori_loop(qi + 1, hi + 1, body, 0)

    o = acc_ref[...] / l_ref[...]
    o_ref[...] = jnp.where(first, o[:blk], o[blk:]).astype(o_ref.dtype)


def _attention(qkv, dims):
    t = dims.tokens
    blk = ATT_BLOCK
    bias, nwin = _attention_bias(blk)
    n_pairs = dims.att_width // LANES
    kernel = functools.partial(_attention_kernel, blk=blk, nwin=nwin, ranges=dims.seq_ranges)
    return pl.pallas_call(
        kernel,
        grid=(n_pairs, t // blk),
        in_specs=[pl.BlockSpec((blk, LANES), lambda h, i: (i, h)),
                  pl.BlockSpec((t, LANES), lambda h, i: (0, n_pairs + h)),
                  pl.BlockSpec((t, LANES), lambda h, i: (0, 2 * n_pairs + h)),
                  pl.BlockSpec(bias.shape, lambda h, i: (0, 0, 0))],
        out_specs=pl.BlockSpec((blk, LANES), lambda h, i: (i, h)),
        out_shape=jax.ShapeDtypeStruct((t, dims.att_width), BF16),
        scratch_shapes=[pltpu.VMEM((2 * blk, LANES), F32), pltpu.VMEM((2 * blk, LANES), F32),
                        pltpu.VMEM((2 * blk, LANES), F32)],
        compiler_params=pltpu.CompilerParams(
            dimension_semantics=("arbitrary", "arbitrary"), vmem_limit_bytes=VMEM_LIMIT),
        name="dilated_attention",
    )(qkv, qkv, qkv, jnp.asarray(bias))


def _split2(x):
    hi = x.astype(BF16)
    return hi, (x - hi.astype(F32)).astype(BF16)


def _head_sum(x, ones_ref):
    ones = ones_ref[...]
    outs = []
    for g in range(x.shape[1] // GROUP_LANES):
        hi, lo = _split2(x[:, g * GROUP_LANES:(g + 1) * GROUP_LANES])
        outs.append(jnp.dot(hi, ones, preferred_element_type=F32)
                    + jnp.dot(lo, ones, preferred_element_type=F32))
    return outs[0] if len(outs) == 1 else jnp.concatenate(outs, axis=1)


def _sigmoid(x):
    return 1.0 / (1.0 + jnp.exp(-x))


def _rw_prep_kernel(*refs, tm, lay, width, first_layer, seq_starts, seq_ends):
    (z_ref, zp_ref, zn_ref, mup_ref, mun_ref, kk_ref, ka_ref, rk_ref, w0_ref, wup_ref, a0_ref,
     aup_ref, gup_ref, ones_ref) = refs[:14]
    rest = refs[14:]
    if first_layer:
        outs = rest
    else:
        vres_ref, v0_ref, vup_ref, vfirst_ref = rest[:4]
        outs = rest[4:]
    (r_o, v_o, kkn_o, kd0_o, b0_o, lw0_o, kd1_o, b1_o, lw1_o, bonus_o, gate_o) = outs[:11]

    i = pl.program_id(0)
    at_start = _is_any(i * tm, seq_starts)
    at_end = _is_any((i + 1) * tm, seq_ends)
    row = lax.broadcasted_iota(jnp.int32, (tm, 1), 0)

    def shifted(name):
        _, dst, _, npad = lay[name]
        cols = slice(dst, dst + npad)
        z = z_ref[:, cols]
        prev_row = jnp.where(at_start, 0.0, zp_ref[7:8, cols])
        next_row = jnp.where(at_end, 0.0, zn_ref[0:1, cols])
        zp = jnp.where(row == 0, prev_row, pltpu.roll(z, 1, 0))
        zn = jnp.where(row == tm - 1, next_row, pltpu.roll(z, tm - 1, 0))
        return z + mup_ref[:, cols] * (zp - z) + mun_ref[:, cols] * (zn - z)

    r = shifted("r")
    kr = shifted("k")
    vr = shifted("v")

    kk = kr * kk_ref[...]
    norm = jnp.sqrt(_head_sum(kk * kk, ones_ref))
    kk = kk / jnp.maximum(norm, 1e-12)

    if first_layer:
        v = vr
        outs[11][...] = vr
    else:
        gate_v = _sigmoid(v0_ref[...] + jnp.dot(vres_ref[...].astype(BF16), vup_ref[...],
                                                preferred_element_type=F32))
        v = vr + (vfirst_ref[...] - vr) * gate_v

    r_o[...] = r.astype(BF16)
    v_o[...] = v.astype(BF16)
    kkn_o[...] = kk.astype(BF16)

    bonus = jnp.zeros((tm, width), F32)
    for d, (kd_o, b_o, lw_o) in enumerate(((kd0_o, b0_o, lw0_o), (kd1_o, b1_o, lw1_o))):
        wd = shifted("wd%d" % d)
        ad = shifted("ad%d" % d)
        wl = w0_ref[d] + jnp.dot(jnp.tanh(wd).astype(BF16), wup_ref[d], preferred_element_type=F32)
        lw_o[...] = -_sigmoid(wl) * math.exp(-0.5)
        a = _sigmoid(a0_ref[d] + jnp.dot(ad.astype(BF16), aup_ref[d], preferred_element_type=F32))
        kd = kr * (1.0 + (a - 1.0) * ka_ref[...])
        kd_o[...] = kd.astype(BF16)
        b_o[...] = (kk * a).astype(BF16)
        bonus = bonus + _head_sum(r * kd * rk_ref[...], ones_ref) * v
    bonus_o[...] = bonus

    gd = shifted("gd")
    gate_o[...] = jnp.dot(_sigmoid(gd).astype(BF16), gup_ref[...], preferred_element_type=F32)


def _rw_prep(z, p, v_first, dims, *, tm, ones):
    t = dims.tokens
    w = dims.rw_width
    lay, _, ncols = dims.rw_layout
    first_layer = v_first is None
    starts = tuple(a for a, _ in dims.seq_ranges)
    ends = tuple(b for _, b in dims.seq_ranges)
    nb8 = t // 8

    def full(shape):
        return pl.BlockSpec(shape, lambda i: (0,) * len(shape))

    in_specs = [
        pl.BlockSpec((tm, ncols), lambda i: (i, 0)),
        pl.BlockSpec((8, ncols), lambda i: (jnp.maximum(i * (tm // 8) - 1, 0), 0)),
        pl.BlockSpec((8, ncols), lambda i: (jnp.minimum((i + 1) * (tm // 8), nb8 - 1), 0)),
        full((1, ncols)), full((1, ncols)), full((1, w)), full((1, w)), full((1, w)),
        full((2, 1, w)), full(p["wup"].shape), full((2, 1, w)), full(p["aup"].shape),
        full(p["gup"].shape), full(ones.shape)]
    args = [z, z, z, p["mu_prev"], p["mu_next"], p["k_k"], p["k_a"], p["r_k"], p["w0"], p["wup"],
            p["a0"], p["aup"], p["gup"], ones]
    if not first_layer:
        in_specs += [pl.BlockSpec((tm, LANES), lambda i: (i, ncols // LANES)),
                     full((1, w)), full(p["vup"].shape), pl.BlockSpec((tm, w), lambda i: (i, 0))]
        args += [z, p["v0"], p["vup"], v_first]

    tok = pl.BlockSpec((tm, w), lambda i: (i, 0))
    dts = [BF16, BF16, BF16, BF16, BF16, F32, BF16, BF16, F32, F32, F32]
    if first_layer:
        dts.append(F32)
    kernel = functools.partial(_rw_prep_kernel, tm=tm, lay=lay, width=w, first_layer=first_layer,
                               seq_starts=starts, seq_ends=ends)
    return pl.pallas_call(
        kernel,
        grid=(t // tm,),
        in_specs=in_specs,
        out_specs=[tok] * len(dts),
        out_shape=[jax.ShapeDtypeStruct((t, w), dt) for dt in dts],
        compiler_params=pltpu.CompilerParams(
            dimension_semantics=("arbitrary",), vmem_limit_bytes=VMEM_LIMIT),
        name="rw_prep",
    )(*args)


def _stack_heads(x, bdm):
    tiled = jnp.concatenate([x] * HEADS_PER_GROUP, axis=0)
    return jnp.where(bdm, tiled, jnp.zeros_like(tiled))


def _dot(a, b):
    return jnp.dot(a, b, preferred_element_type=F32)


def _dot_nt(a, b):
    return lax.dot_general(a, b, (((1,), (1,)), ((), ())), preferred_element_type=F32)


def _dot_tn(a, b):
    return lax.dot_general(a, b, (((0,), (0,)), ((), ())), preferred_element_type=F32)


def _scan_masks(reverse):
    c, g = CHUNK, GROUP_LANES
    row = lax.broadcasted_iota(jnp.int32, (c, g), 0)
    s_idx = lax.broadcasted_iota(jnp.int32, (c, g), 1) % c
    r2 = lax.broadcasted_iota(jnp.int32, (c, c), 0)
    c2 = lax.broadcasted_iota(jnp.int32, (c, c), 1)
    if reverse:
        strict, incl, tri = s_idx > row, s_idx >= row, c2 >= r2
    else:
        strict, incl, tri = s_idx < row, s_idx <= row, c2 <= r2
    ident = jnp.where(s_idx == row, 1.0, 0.0).astype(F32)
    tri = jnp.where(tri, 1.0, 0.0).astype(BF16)
    return strict, incl, ident, tri


def _scan_chunk(r, kd, v, kk, b, lw, state, masks, bdm, eye, reverse):
    strict, incl, ident, tri = masks
    c = CHUNK
    r, kd, kk, b = (x.astype(F32) for x in (r, kd, kk, b))

    hi = lw.astype(BF16)
    rem = lw - hi.astype(F32)
    mid = rem.astype(BF16)
    low = (rem - mid.astype(F32)).astype(BF16)
    cum = _dot(tri, hi) + _dot(tri, mid) + _dot(tri, low)
    tot = cum[0:1] if reverse else cum[c - 1:c]

    e_neg = jnp.exp(-cum)
    e_end = jnp.exp(tot - cum)
    rt_f = r * jnp.exp(cum)
    rt = rt_f.astype(BF16)
    at = (-(kk * jnp.exp(cum - lw))).astype(BF16)
    bt = (b * e_neg).astype(BF16)
    kt = (kd * e_neg).astype(BF16)
    bh = (b * e_end).astype(BF16)
    kh = (kd * e_end).astype(BF16)
    p_end = jnp.exp(tot)

    lhs = jnp.concatenate([at, rt], axis=0)
    g1 = _dot_nt(lhs, _stack_heads(bt, bdm))
    g2 = _dot_nt(lhs, _stack_heads(kt, bdm))
    a_ab = jnp.where(strict, g1[:c], 0.0)
    b_rb = jnp.where(incl, g1[c:], 0.0).astype(BF16)
    a_ak = jnp.where(strict, g2[:c], 0.0)
    b_rk = jnp.where(incl, g2[c:], 0.0).astype(BF16)

    t_inv = ident + a_ab
    power = _dot(a_ab.astype(BF16), _stack_heads(a_ab.astype(BF16), bdm))
    for _ in range(4):
        both = _dot(jnp.concatenate([power, t_inv], axis=0).astype(BF16),
                    _stack_heads(power.astype(BF16), bdm))
        power, t_inv = both[:c], t_inv + both[c:]
    t_inv = t_inv + _dot(t_inv.astype(BF16), _stack_heads(power.astype(BF16), bdm))
    t_b = t_inv.astype(BF16)

    v_stack = _stack_heads(v, bdm)
    t_ak = _dot(t_b, _stack_heads(a_ak.astype(BF16), bdm)).astype(BF16)
    a_p = _dot(t_b, _stack_heads(at, bdm)).astype(BF16)
    u0 = _dot(t_ak, v_stack).astype(BF16)

    r_p = rt_f + _dot(b_rb, _stack_heads(a_p, bdm))
    y0 = _dot(b_rb, _stack_heads(u0, bdm)) + _dot(b_rk, v_stack)

    m_bd = jnp.where(bdm, _dot_tn(a_p, bh), 0.0) + jnp.where(eye, p_end, 0.0)
    n_full = jnp.where(bdm, _dot_tn(u0, bh) + _dot_tn(v, kh), 0.0)
    n_ls = n_full[0:c]
    for h in range(1, HEADS_PER_GROUP):
        n_ls = n_ls + n_full[h * c:(h + 1) * c]

    s_b = state.astype(BF16)
    y = _dot_nt(r_p.astype(BF16), _stack_heads(s_b, bdm)) + y0
    new_state = _dot(s_b, m_bd.astype(BF16)) + n_ls
    return y, new_state


def _wkv_scan_kernel(rf, vf, kkf, kdf, bf, lwf, rb, vb, kkb, kdb, bb, lwb, yf_ref, yb_ref,
                     sf_ref, sb_ref, *, seq_start_blocks):
    step = pl.program_id(1)

    @pl.when(_is_any(step, seq_start_blocks))
    def _():
        sf_ref[...] = jnp.zeros_like(sf_ref)
        sb_ref[...] = jnp.zeros_like(sb_ref)

    g = GROUP_LANES
    bdm = (lax.broadcasted_iota(jnp.int32, (g, g), 0) // HEAD_DIM
           == lax.broadcasted_iota(jnp.int32, (g, g), 1) // HEAD_DIM)
    eye = (lax.broadcasted_iota(jnp.int32, (g, g), 0)
           == lax.broadcasted_iota(jnp.int32, (g, g), 1))
    masks_f = _scan_masks(False)
    masks_b = _scan_masks(True)
    n_chunks = SCAN_BLOCK // CHUNK

    def body(ci, carry):
        rows = pl.ds(pl.multiple_of(ci * CHUNK, CHUNK), CHUNK)
        y, s = _scan_chunk(rf[rows, :], kdf[rows, :], vf[rows, :], kkf[rows, :], bf[rows, :],
                           lwf[rows, :], sf_ref[...], masks_f, bdm, eye, False)
        yf_ref[rows, :] = y
        sf_ref[...] = s
        rows = pl.ds(pl.multiple_of((n_chunks - 1 - ci) * CHUNK, CHUNK), CHUNK)
        y, s = _scan_chunk(rb[rows, :], kdb[rows, :], vb[rows, :], kkb[rows, :], bb[rows, :],
                           lwb[rows, :], sb_ref[...], masks_b, bdm, eye, True)
        yb_ref[rows, :] = y
        sb_ref[...] = s
        return carry

    lax.fori_loop(0, n_chunks, body, 0)


def _wkv_scan(r, v, kk, kd0, b0, lw0, kd1, b1, lw1, dims):
    t = dims.tokens
    w = dims.rw_width
    blk = SCAN_BLOCK
    ranges = dims.seq_ranges
    n_groups = w // GROUP_LANES

    def rev_block(s):
        lo, hi = _seg_bounds(s, ranges, blk)
        return lo + hi - 1 - s

    fwd = pl.BlockSpec((blk, GROUP_LANES), lambda g, s: (s, g))
    bwd = pl.BlockSpec((blk, GROUP_LANES), lambda g, s: (rev_block(s), g))
    kernel = functools.partial(_wkv_scan_kernel,
                               seq_start_blocks=tuple(a // blk for a, _ in ranges))
    return pl.pallas_call(
        kernel,
        grid=(n_groups, t // blk),
        in_specs=[fwd] * 6 + [bwd] * 6,
        out_specs=[fwd, bwd],
        out_shape=[jax.ShapeDtypeStruct((t, w), F32)] * 2,
        scratch_shapes=[pltpu.VMEM((HEAD_DIM, GROUP_LANES), F32)] * 2,
        compiler_params=pltpu.CompilerParams(
            dimension_semantics=("arbitrary", "arbitrary"), vmem_limit_bytes=VMEM_LIMIT),
        name="wkv_scan",
    )(r, v, kk, kd0, b0, lw0, r, v, kk, kd1, b1, lw1)


def _rw_post_kernel(yf_ref, yb_ref, bonus_ref, gate_ref, g_ref, b_ref, ones_ref, o_ref):
    y = yf_ref[...] + yb_ref[...]
    mu = _head_sum(y, ones_ref) * (1.0 / HEAD_DIM)
    d = y - mu
    var = _head_sum(d * d, ones_ref) * (1.0 / HEAD_DIM)
    yn = d * lax.rsqrt(var + GN_EPS) * g_ref[...] + b_ref[...]
    o_ref[...] = ((yn + bonus_ref[...]) * gate_ref[...]).astype(o_ref.dtype)


def _rw_post(yf, yb, bonus, gate, g, b, ones, *, tm):
    t, w = yf.shape
    tok = pl.BlockSpec((tm, w), lambda i: (i, 0))
    vec = pl.BlockSpec((1, w), lambda i: (0, 0))
    return pl.pallas_call(
        _rw_post_kernel,
        grid=(t // tm,),
        in_specs=[tok, tok, tok, tok, vec, vec, pl.BlockSpec(ones.shape, lambda i: (0, 0))],
        out_specs=tok,
        out_shape=jax.ShapeDtypeStruct((t, w), BF16),
        compiler_params=pltpu.CompilerParams(
            dimension_semantics=("arbitrary",), vmem_limit_bytes=VMEM_LIMIT),
        name="rw_post",
    )(yf, yb, bonus, gate, g, b, ones)


def _layer_norm_rows(h, g_ref, b_ref):
    mu = jnp.mean(h, axis=-1, keepdims=True)
    d = h - mu
    var = jnp.mean(d * d, axis=-1, keepdims=True)
    return d * lax.rsqrt(var + LN_EPS) * g_ref[...] + b_ref[...]


def _out_proj_kernel(att_ref, rw_ref, w_ref, x_ref, g_ref, b_ref, o_ref, *, n_half, alpha):
    k = pl.program_id(1)

    @pl.when(k == 0)
    def _():
        o_ref[...] = alpha * x_ref[...]

    @pl.when(k < n_half)
    def _():
        o_ref[...] += jnp.dot(att_ref[...], w_ref[...], preferred_element_type=F32)

    @pl.when(k >= n_half)
    def _():
        o_ref[...] += jnp.dot(rw_ref[...], w_ref[...], preferred_element_type=F32)

    @pl.when(k == 2 * n_half - 1)
    def _():
        o_ref[...] = _layer_norm_rows(o_ref[...], g_ref, b_ref)


def _out_proj_ln(att, rw, w, x, g, b, dims, *, tm, tk):
    t, d = x.shape
    n_half = dims.att_width // tk
    vec = pl.BlockSpec((1, d), lambda i, k: (0, 0))
    return pl.pallas_call(
        functools.partial(_out_proj_kernel, n_half=n_half, alpha=dims.alpha),
        grid=(t // tm, 2 * n_half),
        in_specs=[pl.BlockSpec((tm, tk), lambda i, k: (i, jnp.minimum(k, n_half - 1))),
                  pl.BlockSpec((tm, tk), lambda i, k: (i, jnp.maximum(k - n_half, 0))),
                  pl.BlockSpec((tk, d), lambda i, k: (k, 0)),
                  pl.BlockSpec((tm, d), lambda i, k: (i, 0)), vec, vec],
        out_specs=pl.BlockSpec((tm, d), lambda i, k: (i, 0)),
        out_shape=jax.ShapeDtypeStruct((t, d), F32),
        compiler_params=pltpu.CompilerParams(
            dimension_semantics=("arbitrary", "arbitrary"), vmem_limit_bytes=VMEM_LIMIT),
        name="out_proj_ln",
    )(att, rw, w, x, g, b)


def _mlp_kernel(x_ref, w1_ref, w2_ref, g_ref, b_ref, o_ref, xb_ref, *, alpha):
    f = pl.program_id(1)

    @pl.when(f == 0)
    def _():
        x = x_ref[...]
        xb_ref[...] = x.astype(BF16)
        o_ref[...] = alpha * x

    a = jnp.dot(xb_ref[...], w1_ref[...], preferred_element_type=F32)
    a = jnp.square(jnp.maximum(a, 0.0)).astype(BF16)
    o_ref[...] += jnp.dot(a, w2_ref[...], preferred_element_type=F32)

    @pl.when(f == pl.num_programs(1) - 1)
    def _():
        o_ref[...] = _layer_norm_rows(o_ref[...], g_ref, b_ref)


def _mlp_ln(x, w1, w2, g, b, dims, *, tm, tf):
    t, d = x.shape
    vec = pl.BlockSpec((1, d), lambda i, f: (0, 0))
    return pl.pallas_call(
        functools.partial(_mlp_kernel, alpha=dims.alpha),
        grid=(t // tm, dims.d_ff // tf),
        in_specs=[pl.BlockSpec((tm, d), lambda i, f: (i, 0)),
                  pl.BlockSpec((d, tf), lambda i, f: (0, f)),
                  pl.BlockSpec((tf, d), lambda i, f: (f, 0)), vec, vec],
        out_specs=pl.BlockSpec((tm, d), lambda i, f: (i, 0)),
        out_shape=jax.ShapeDtypeStruct((t, d), F32),
        scratch_shapes=[pltpu.VMEM((tm, d), BF16)],
        compiler_params=pltpu.CompilerParams(
            dimension_semantics=("arbitrary", "arbitrary"), vmem_limit_bytes=VMEM_LIMIT),
        name="mlp_ln",
    )(x, w1, w2, g, b)


def _pad_cols(a, n):
    return jnp.pad(a, [(0, 0)] * (a.ndim - 1) + [(0, n - a.shape[-1])])


def _pad_rows(a, n):
    return jnp.pad(a, [(0, 0)] * (a.ndim - 2) + [(0, n - a.shape[-2]), (0, 0)])


def _rw_columns(a, dims):
    lay, _, _ = dims.rw_layout
    parts = [_pad_cols(a[..., src:src + n], npad) for src, _, n, npad in lay.values()]
    return jnp.concatenate(parts, axis=-1)


def _layer_params(P, l, dims):
    att_cols = 3 * dims.att_width
    w_in = P["w_in"][l]
    w_rw = _rw_columns(w_in[:, att_cols:], dims)
    if l > 0:
        w_rw = jnp.concatenate([w_rw, _pad_cols(P["w_vres_in"][l - 1], LANES)], axis=1)
    w_rw = _pad_cols(w_rw, _round_up(w_rw.shape[1], 512))
    row = lambda a: a.reshape(1, -1)
    p = dict(
        w_qkv=w_in[:, :att_cols].astype(BF16),
        w_rw=w_rw.astype(BF16),
        mu_prev=row(_rw_columns(P["mu_prev"][l], dims)),
        mu_next=row(_rw_columns(P["mu_next"][l], dims)),
        k_k=row(P["k_k"][l]), k_a=row(P["k_a"][l]), r_k=row(P["r_k"][l]),
        w0=P["w0"][l][:, None, :], a0=P["a0"][l][:, None, :],
        wup=_pad_rows(P["w_lora_up"][l], _round_up(dims.decay_lora, LANES)).astype(BF16),
        aup=_pad_rows(P["a_lora_up"][l], _round_up(dims.aaa_lora, LANES)).astype(BF16),
        gup=_pad_rows(P["g_lora_up"][l], _round_up(dims.gate_lora, LANES)).astype(BF16),
        lnx_g=row(P["lnx_g"][l]), lnx_b=row(P["lnx_b"][l]),
        w_out=P["w_out"][l].astype(BF16),
        ln1_g=row(P["ln1_g"][l]), ln1_b=row(P["ln1_b"][l]),
        w_ff1=P["w_ff1"][l].astype(BF16), w_ff2=P["w_ff2"][l].astype(BF16),
        ln2_g=row(P["ln2_g"][l]), ln2_b=row(P["ln2_b"][l]),
    )
    if l > 0:
        p["v0"] = row(P["v0"][l - 1])
        p["vup"] = _pad_rows(P["v_lora_up"][l - 1], LANES).astype(BF16)
    return p


def _tile(n, pref):
    while n % pref:
        pref //= 2
    return pref


def _trunk(x, P, dims):
    t = dims.tokens
    tm = _tile(t, 512)
    rot = _rotary_tables(dims)
    ones = jnp.asarray(np.kron(np.eye(HEADS_PER_GROUP), np.ones((HEAD_DIM, HEAD_DIM))), BF16)
    v_first = None
    for l in range(dims.depth):
        p = _layer_params(P, l, dims)
        qkv = _proj_qkv(x, p["w_qkv"], *rot, tm=tm, tn=_tile(dims.att_width, 512))
        att = _attention(qkv, dims)
        z = _proj_plain(x, p["w_rw"], tm=tm, tn=512)
        prep = _rw_prep(z, p, v_first, dims, tm=_tile(t, 128), ones=ones)
        r, v, kk, kd0, b0, lw0, kd1, b1, lw1, bonus, gate = prep[:11]
        if l == 0:
            v_first = prep[11]
        yf, yb = _wkv_scan(r, v, kk, kd0, b0, lw0, kd1, b1, lw1, dims)
        rw = _rw_post(yf, yb, bonus, gate, p["lnx_g"], p["lnx_b"], ones, tm=tm)
        x = _out_proj_ln(att, rw, p["w_out"], x, p["ln1_g"], p["ln1_b"], dims,
                         tm=tm, tk=_tile(dims.att_width, 512))
        x = _mlp_ln(x, p["w_ff1"], p["w_ff2"], p["ln2_g"], p["ln2_b"], dims,
                    tm=tm, tf=_tile(dims.d_ff, 256))
    return x


def _lora_dim(scale, power, d_model):
    return max(32, int(round(scale * d_model ** power / 32)) * 32)


def _make_dims(d_model, d_ff, depth, seqs):
    return Dims(d_model=d_model, d_ff=d_ff, depth=depth, seqs=seqs,
                decay_lora=_lora_dim(1.8, 0.5, d_model), aaa_lora=_lora_dim(1.8, 0.5, d_model),
                mv_lora=_lora_dim(1.3, 0.5, d_model), gate_lora=_lora_dim(0.6, 0.8, d_model))


def kernel(x_prompt, x_sample, w_in, w_vres_in, mu_prev, mu_next, w0, w_lora_up, a0, a_lora_up, v0, v_lora_up, g_lora_up, k_k, k_a, r_k, lnx_g, lnx_b, w_out, ln1_g, ln1_b, w_ff1, w_ff2, ln2_g, ln2_b):
    P = dict(w_in=w_in, w_vres_in=w_vres_in, mu_prev=mu_prev, mu_next=mu_next, w0=w0,
             w_lora_up=w_lora_up, a0=a0, a_lora_up=a_lora_up, v0=v0, v_lora_up=v_lora_up,
             g_lora_up=g_lora_up, k_k=k_k, k_a=k_a, r_k=r_k, lnx_g=lnx_g, lnx_b=lnx_b,
             w_out=w_out, ln1_g=ln1_g, ln1_b=ln1_b, w_ff1=w_ff1, w_ff2=w_ff2,
             ln2_g=ln2_g, ln2_b=ln2_b)
    d_model = x_prompt.shape[-1]
    dims = _make_dims(d_model, w_ff1.shape[-1], w_in.shape[0],
                      (tuple(x_prompt.shape[:2]), tuple(x_sample.shape[:2])))
    n_prompt = x_prompt.shape[0] * x_prompt.shape[1]
    x = jnp.concatenate([x_prompt.reshape(-1, d_model), x_sample.reshape(-1, d_model)], axis=0)
    y = _trunk(x, P, dims)
    return y[:n_prompt].reshape(x_prompt.shape), y[n_prompt:].reshape(x_sample.shape)
```

```python
import dataclasses
import functools
import math

import jax
import jax.numpy as jnp
import numpy as np
from jax import lax
from jax.experimental import pallas as pl
from jax.experimental.pallas import tpu as pltpu

F32 = jnp.float32
BF16 = jnp.bfloat16

LANES = 128
HEAD_DIM = 64
HEADS_PER_GROUP = 4
GROUP_LANES = HEADS_PER_GROUP * HEAD_DIM
CHUNK = 64
SCAN_BLOCK = 256
ATT_BLOCK = 256
PROJ_TILE = 512
MLP_TILE = 512
ROT_DIM = HEAD_DIM // 4
ROPE_THETA = 500000.0
DILATED_PATTERNS = ((128, 1), (512, 4), (2048, 16))
LN_EPS = 1e-5
GN_EPS = 64e-5
NEG_INF = -1e30
VMEM_LIMIT = 60 * 1024 * 1024


def _round_up(n, m):
    return (n + m - 1) // m * m


@dataclasses.dataclass(frozen=True)
class Dims:
    d_model: int
    d_ff: int
    depth: int
    seqs: tuple
    decay_lora: int
    aaa_lora: int
    mv_lora: int
    gate_lora: int

    @property
    def att_width(self):
        return self.d_model // 2

    @property
    def rw_width(self):
        return self.d_model - self.att_width

    @property
    def tokens(self):
        return sum(b * s for b, s in self.seqs)

    @property
    def seq_ranges(self):
        out, t = [], 0
        for b, s in self.seqs:
            for _ in range(b):
                out.append((t, t + s))
                t += s
        return tuple(out)

    @property
    def alpha(self):
        return (2 * self.depth) ** 0.25

    @property
    def rw_layout(self):
        w = self.rw_width
        names = (("r", w), ("k", w), ("v", w), ("wd0", self.decay_lora), ("wd1", self.decay_lora),
                 ("ad0", self.aaa_lora), ("ad1", self.aaa_lora), ("gd", self.gate_lora))
        lay, src, dst = {}, 0, 0
        for name, n in names:
            lay[name] = (src, dst, n, _round_up(n, LANES))
            src += n
            dst += _round_up(n, LANES)
        return lay, src, dst


def _positions(dims):
    pos = np.zeros((dims.tokens,), np.int64)
    for lo, hi in dims.seq_ranges:
        pos[lo:hi] = np.arange(hi - lo)
    return pos


def _seg_bounds(i, ranges, unit):
    lo = jnp.int32(0)
    hi = jnp.int32(0)
    for a, b in ranges:
        inside = jnp.logical_and(i >= a // unit, i < b // unit)
        lo = jnp.where(inside, a // unit, lo)
        hi = jnp.where(inside, b // unit, hi)
    return lo, hi


def _is_any(i, values):
    out = i == values[0]
    for v in values[1:]:
        out = jnp.logical_or(out, i == v)
    return out


def _cast_x_once(x_ref, xb_ref):
    @pl.when(pl.program_id(1) == 0)
    def _():
        xb_ref[...] = x_ref[...].astype(BF16)


def _proj_plain_kernel(x_ref, w_ref, o_ref, xb_ref):
    _cast_x_once(x_ref, xb_ref)
    o_ref[...] = jnp.dot(xb_ref[...], w_ref[...], preferred_element_type=F32)


def _proj_plain(x, w, *, tm):
    t, d = x.shape
    nt, _, tn = w.shape
    return pl.pallas_call(
        _proj_plain_kernel,
        grid=(t // tm, nt),
        in_specs=[pl.BlockSpec((tm, d), lambda i, j: (i, 0)),
                  pl.BlockSpec((None, d, tn), lambda i, j: (j, 0, 0))],
        out_specs=pl.BlockSpec((tm, tn), lambda i, j: (i, j)),
        out_shape=jax.ShapeDtypeStruct((t, nt * tn), F32),
        scratch_shapes=[pltpu.VMEM((tm, d), BF16)],
        compiler_params=pltpu.CompilerParams(
            dimension_semantics=("arbitrary", "arbitrary"), vmem_limit_bytes=VMEM_LIMIT),
        name="proj_plain",
    )(x, w)


def _proj_qkv_kernel(x_ref, w_ref, c_ref, s1_ref, s2_ref, o_ref, xb_ref, *, tn, width):
    _cast_x_once(x_ref, xb_ref)
    j = pl.program_id(1)
    acc = jnp.dot(xb_ref[...], w_ref[...], preferred_element_type=F32)
    tiles_per_part = width // tn

    def rotated(scale):
        c, s1, s2 = c_ref[...], s1_ref[...], s2_ref[...]
        for g in range(tn // LANES):
            a = acc[:, g * LANES:(g + 1) * LANES]
            rot = (a * c + pltpu.roll(a, ROT_DIM // 2, 1) * s1
                   + pltpu.roll(a, LANES - ROT_DIM // 2, 1) * s2)
            if scale != 1.0:
                rot = rot * scale
            o_ref[:, g * LANES:(g + 1) * LANES] = rot.astype(o_ref.dtype)

    @pl.when(j < tiles_per_part)
    def _():
        rotated(HEAD_DIM ** -0.5)

    @pl.when(jnp.logical_and(j >= tiles_per_part, j < 2 * tiles_per_part))
    def _():
        rotated(1.0)

    @pl.when(j >= 2 * tiles_per_part)
    def _():
        o_ref[...] = acc.astype(o_ref.dtype)


def _proj_qkv(x, w, rot_c, rot_s1, rot_s2, *, tm):
    t, d = x.shape
    nt, _, tn = w.shape
    n = nt * tn
    width = n // 3
    tab = pl.BlockSpec((tm, LANES), lambda i, j: (i, 0))
    return pl.pallas_call(
        functools.partial(_proj_qkv_kernel, tn=tn, width=width),
        grid=(t // tm, nt),
        in_specs=[pl.BlockSpec((tm, d), lambda i, j: (i, 0)),
                  pl.BlockSpec((None, d, tn), lambda i, j: (j, 0, 0)), tab, tab, tab],
        out_specs=pl.BlockSpec((tm, tn), lambda i, j: (i, j)),
        out_shape=jax.ShapeDtypeStruct((t, n), BF16),
        scratch_shapes=[pltpu.VMEM((tm, d), BF16)],
        compiler_params=pltpu.CompilerParams(
            dimension_semantics=("arbitrary", "arbitrary"), vmem_limit_bytes=VMEM_LIMIT),
        name="proj_qkv",
    )(x, w, rot_c, rot_s1, rot_s2)


def _rotary_tables(dims):
    half = ROT_DIM // 2
    pos = jnp.asarray(_positions(dims), F32)
    inv_freq = jnp.power(ROPE_THETA, -jnp.arange(half, dtype=F32) * 2.0 / ROT_DIM)
    ang = pos[:, None] * inv_freq[None, :]
    cos, sin = jnp.cos(ang), jnp.sin(ang)
    t = dims.tokens
    ones = jnp.ones((t, HEAD_DIM - ROT_DIM), F32)
    zeros_rest = jnp.zeros((t, HEAD_DIM - ROT_DIM), F32)
    zeros_half = jnp.zeros((t, half), F32)
    c = jnp.concatenate([cos, cos, ones], axis=1)
    s1 = jnp.concatenate([zeros_half, sin, zeros_rest], axis=1)
    s2 = jnp.concatenate([-sin, zeros_half, zeros_rest], axis=1)
    rep = LANES // HEAD_DIM
    return tuple(jnp.tile(a, (1, rep)) for a in (c, s1, s2))


def _attention_bias(blk):
    reach = max(w // 2 for w, _ in DILATED_PATTERNS)
    nwin = -(-reach // blk)
    qi = np.arange(blk)[:, None]
    ki = np.arange(blk)[None, :]
    tabs = []
    for j in range(-nwin, nwin + 1):
        d = j * blk + ki - qi
        mult = np.zeros(d.shape, np.int64)
        for window, dil in DILATED_PATTERNS:
            half = window // (2 * dil)
            mult += ((d % dil == 0) & (np.abs(d) <= half * dil)).astype(np.int64)
        with np.errstate(divide="ignore"):
            tabs.append(np.where(mult > 0, np.log(np.maximum(mult, 1)), NEG_INF).T)
    return np.stack(tabs).astype(np.float32), nwin


def _attention_kernel(q_ref, k_ref, v_ref, bias_ref, o_ref, *, blk, nwin, ranges):
    qi = pl.program_id(1)
    seg_lo, seg_hi = _seg_bounds(qi, ranges, blk)

    q = q_ref[...]
    first = lax.broadcasted_iota(jnp.int32, q.shape, 1) < HEAD_DIM
    zero = jnp.zeros_like(q)
    qs = jnp.concatenate([jnp.where(first, q, zero), jnp.where(first, zero, q)], axis=0)

    scores, starts = [], []
    for j in range(2 * nwin + 1):
        kb = qi + (j - nwin)
        valid = jnp.logical_and(kb >= seg_lo, kb < seg_hi)
        start = pl.multiple_of(jnp.clip(kb, seg_lo, seg_hi - 1) * blk, blk)
        s = _dot_nt(k_ref[pl.ds(start, blk), :], qs)
        b = jnp.where(valid, bias_ref[j], NEG_INF)
        scores.append(s + jnp.concatenate([b, b], axis=1))
        starts.append(start)

    m = jnp.max(scores[0], axis=0, keepdims=True)
    for s in scores[1:]:
        m = jnp.maximum(m, jnp.max(s, axis=0, keepdims=True))
    denom = jnp.zeros_like(m)
    acc = jnp.zeros((LANES, 2 * blk), F32)
    for s, start in zip(scores, starts):
        p = jnp.exp(s - m)
        denom = denom + jnp.sum(p, axis=0, keepdims=True)
        acc = acc + _dot_tn(v_ref[pl.ds(start, blk), :], p.astype(BF16))
    o = acc / denom
    head0 = lax.broadcasted_iota(jnp.int32, (LANES, blk), 0) < HEAD_DIM
    o_ref[...] = jnp.where(head0, o[:, :blk], o[:, blk:]).T.astype(o_ref.dtype)


def _attention(qkv, dims):
    t = dims.tokens
    blk = ATT_BLOCK
    bias, nwin = _attention_bias(blk)
    n_pairs = dims.att_width // LANES
    kernel = functools.partial(_attention_kernel, blk=blk, nwin=nwin, ranges=dims.seq_ranges)
    return pl.pallas_call(
        kernel,
        grid=(n_pairs, t // blk),
        in_specs=[pl.BlockSpec((blk, LANES), lambda h, i: (i, h)),
                  pl.BlockSpec((t, LANES), lambda h, i: (0, n_pairs + h)),
                  pl.BlockSpec((t, LANES), lambda h, i: (0, 2 * n_pairs + h)),
                  pl.BlockSpec(bias.shape, lambda h, i: (0, 0, 0))],
        out_specs=pl.BlockSpec((blk, LANES), lambda h, i: (i, h)),
        out_shape=jax.ShapeDtypeStruct((t, dims.att_width), BF16),
        compiler_params=pltpu.CompilerParams(
            dimension_semantics=("arbitrary", "arbitrary"), vmem_limit_bytes=VMEM_LIMIT),
        name="dilated_attention",
    )(qkv, qkv, qkv, jnp.asarray(bias))


def _split2(x):
    hi = x.astype(BF16)
    return hi, (x - hi.astype(F32)).astype(BF16)


def _head_sum(x, ones_ref):
    ones = ones_ref[...]
    outs = []
    for g in range(x.shape[1] // GROUP_LANES):
        hi, lo = _split2(x[:, g * GROUP_LANES:(g + 1) * GROUP_LANES])
        outs.append(jnp.dot(hi, ones, preferred_element_type=F32)
                    + jnp.dot(lo, ones, preferred_element_type=F32))
    return outs[0] if len(outs) == 1 else jnp.concatenate(outs, axis=1)


def _sigmoid(x):
    return 1.0 / (1.0 + jnp.exp(-x))


def _rw_prep_kernel(*refs, tm, lay, width, first_layer, seq_starts, seq_ends):
    (z_ref, zp_ref, zn_ref, mup_ref, mun_ref, kk_ref, ka_ref, rk_ref, w0_ref, wup_ref, a0_ref,
     aup_ref, gup_ref, ones_ref) = refs[:14]
    rest = refs[14:]
    if first_layer:
        outs = rest
    else:
        vres_ref, v0_ref, vup_ref, vfirst_ref = rest[:4]
        outs = rest[4:]
    (r_o, v_o, kkn_o, kd0_o, b0_o, lw0_o, kd1_o, b1_o, lw1_o, bonus_o, gate_o) = outs[:11]

    i = pl.program_id(0)
    at_start = _is_any(i * tm, seq_starts)
    at_end = _is_any((i + 1) * tm, seq_ends)
    row = lax.broadcasted_iota(jnp.int32, (tm, 1), 0)

    def shifted(name):
        _, dst, _, npad = lay[name]
        cols = slice(dst, dst + npad)
        z = z_ref[:, cols]
        prev_row = jnp.where(at_start, 0.0, zp_ref[7:8, cols])
        next_row = jnp.where(at_end, 0.0, zn_ref[0:1, cols])
        zp = jnp.where(row == 0, prev_row, pltpu.roll(z, 1, 0))
        zn = jnp.where(row == tm - 1, next_row, pltpu.roll(z, tm - 1, 0))
        return z + mup_ref[:, cols] * (zp - z) + mun_ref[:, cols] * (zn - z)

    r = shifted("r")
    kr = shifted("k")
    vr = shifted("v")

    kk = kr * kk_ref[...]
    norm = jnp.sqrt(_head_sum(kk * kk, ones_ref))
    kk = kk / jnp.maximum(norm, 1e-12)

    if first_layer:
        v = vr
        outs[11][...] = vr
    else:
        gate_v = _sigmoid(v0_ref[...] + jnp.dot(vres_ref[...].astype(BF16), vup_ref[...],
                                                preferred_element_type=F32))
        v = vr + (vfirst_ref[...] - vr) * gate_v

    r_o[...] = r.astype(BF16)
    v_o[...] = v.astype(BF16)
    kkn_o[...] = kk.astype(BF16)

    bonus = jnp.zeros((tm, width), F32)
    for d, (kd_o, b_o, lw_o) in enumerate(((kd0_o, b0_o, lw0_o), (kd1_o, b1_o, lw1_o))):
        wd = shifted("wd%d" % d)
        ad = shifted("ad%d" % d)
        wl = w0_ref[d] + jnp.dot(jnp.tanh(wd).astype(BF16), wup_ref[d], preferred_element_type=F32)
        lw_o[...] = -_sigmoid(wl) * math.exp(-0.5)
        a = _sigmoid(a0_ref[d] + jnp.dot(ad.astype(BF16), aup_ref[d], preferred_element_type=F32))
        kd = kr * (1.0 + (a - 1.0) * ka_ref[...])
        kd_o[...] = kd.astype(BF16)
        b_o[...] = (kk * a).astype(BF16)
        bonus = bonus + _head_sum(r * kd * rk_ref[...], ones_ref) * v
    bonus_o[...] = bonus

    gd = shifted("gd")
    gate_o[...] = jnp.dot(_sigmoid(gd).astype(BF16), gup_ref[...], preferred_element_type=F32)


def _rw_prep(z, p, v_first, dims, *, tm, ones):
    t = dims.tokens
    w = dims.rw_width
    lay, _, ncols = dims.rw_layout
    first_layer = v_first is None
    starts = tuple(a for a, _ in dims.seq_ranges)
    ends = tuple(b for _, b in dims.seq_ranges)
    nb8 = t // 8

    def full(shape):
        return pl.BlockSpec(shape, lambda i: (0,) * len(shape))

    in_specs = [
        pl.BlockSpec((tm, ncols), lambda i: (i, 0)),
        pl.BlockSpec((8, ncols), lambda i: (jnp.maximum(i * (tm // 8) - 1, 0), 0)),
        pl.BlockSpec((8, ncols), lambda i: (jnp.minimum((i + 1) * (tm // 8), nb8 - 1), 0)),
        full((1, ncols)), full((1, ncols)), full((1, w)), full((1, w)), full((1, w)),
        full((2, 1, w)), full(p["wup"].shape), full((2, 1, w)), full(p["aup"].shape),
        full(p["gup"].shape), full(ones.shape)]
    args = [z, z, z, p["mu_prev"], p["mu_next"], p["k_k"], p["k_a"], p["r_k"], p["w0"], p["wup"],
            p["a0"], p["aup"], p["gup"], ones]
    if not first_layer:
        in_specs += [pl.BlockSpec((tm, LANES), lambda i: (i, ncols // LANES)),
                     full((1, w)), full(p["vup"].shape), pl.BlockSpec((tm, w), lambda i: (i, 0))]
        args += [z, p["v0"], p["vup"], v_first]

    tok = pl.BlockSpec((tm, w), lambda i: (i, 0))
    dts = [BF16, BF16, BF16, BF16, BF16, F32, BF16, BF16, F32, F32, F32]
    if first_layer:
        dts.append(F32)
    kernel = functools.partial(_rw_prep_kernel, tm=tm, lay=lay, width=w, first_layer=first_layer,
                               seq_starts=starts, seq_ends=ends)
    return pl.pallas_call(
        kernel,
        grid=(t // tm,),
        in_specs=in_specs,
        out_specs=[tok] * len(dts),
        out_shape=[jax.ShapeDtypeStruct((t, w), dt) for dt in dts],
        compiler_params=pltpu.CompilerParams(
            dimension_semantics=("arbitrary",), vmem_limit_bytes=VMEM_LIMIT),
        name="rw_prep",
    )(*args)


def _stack_heads(x, bdm):
    tiled = jnp.concatenate([x] * HEADS_PER_GROUP, axis=0)
    return jnp.where(bdm, tiled, jnp.zeros_like(tiled))


def _dot(a, b):
    return jnp.dot(a, b, preferred_element_type=F32)


def _dot_nt(a, b):
    return lax.dot_general(a, b, (((1,), (1,)), ((), ())), preferred_element_type=F32)


def _dot_tn(a, b):
    return lax.dot_general(a, b, (((0,), (0,)), ((), ())), preferred_element_type=F32)


def _scan_masks(reverse):
    c, g = CHUNK, GROUP_LANES
    row = lax.broadcasted_iota(jnp.int32, (c, g), 0)
    s_idx = lax.broadcasted_iota(jnp.int32, (c, g), 1) % c
    r2 = lax.broadcasted_iota(jnp.int32, (c, c), 0)
    c2 = lax.broadcasted_iota(jnp.int32, (c, c), 1)
    if reverse:
        strict, incl, tri = s_idx > row, s_idx >= row, c2 >= r2
    else:
        strict, incl, tri = s_idx < row, s_idx <= row, c2 <= r2
    ident = jnp.where(s_idx == row, 1.0, 0.0).astype(F32)
    tri = jnp.where(tri, 1.0, 0.0).astype(BF16)
    return strict, incl, ident, tri


def _each(fn, *lists):
    return [fn(*args) for args in zip(*lists)]


def _scan_prepare(ops, masks, reverse, bdm, eye):
    strict, incl, ident, tri = ([m[i] for m in masks] for i in range(4))
    c = CHUNK
    stack = lambda x: _stack_heads(x, bdm)
    bf = lambda x: x.astype(BF16)
    r, kd, v, kk, b, lw = ([o[i] for o in ops] for i in range(6))
    r, kd, kk, b = ([x.astype(F32) for x in xs] for xs in (r, kd, kk, b))

    hi = _each(bf, lw)
    rem = _each(lambda x, h: x - h.astype(F32), lw, hi)
    mid = _each(bf, rem)
    low = _each(lambda x, m: (x - m.astype(F32)).astype(BF16), rem, mid)
    cum = _each(lambda t, h, m, l: _dot(t, h) + _dot(t, m) + _dot(t, l), tri, hi, mid, low)
    tot = [x[0:1] if rev else x[c - 1:c] for x, rev in zip(cum, reverse)]

    e_neg = _each(lambda x: jnp.exp(-x), cum)
    e_end = _each(lambda t, x: jnp.exp(t - x), tot, cum)
    rt_f = _each(lambda x, e: x * jnp.exp(e), r, cum)
    rt = _each(bf, rt_f)
    at = _each(lambda x, e, w: (-(x * jnp.exp(e - w))).astype(BF16), kk, cum, lw)
    bt = _each(lambda x, e: (x * e).astype(BF16), b, e_neg)
    kt = _each(lambda x, e: (x * e).astype(BF16), kd, e_neg)
    bh = _each(lambda x, e: (x * e).astype(BF16), b, e_end)
    kh = _each(lambda x, e: (x * e).astype(BF16), kd, e_end)
    p_end = _each(jnp.exp, tot)

    lhs = _each(lambda x, y: jnp.concatenate([x, y], axis=0), at, rt)
    g1 = _each(lambda x, y: _dot_nt(x, stack(y)), lhs, bt)
    g2 = _each(lambda x, y: _dot_nt(x, stack(y)), lhs, kt)
    a_ab = _each(lambda m, x: jnp.where(m, x[:c], 0.0), strict, g1)
    b_rb = _each(lambda m, x: jnp.where(m, x[c:], 0.0).astype(BF16), incl, g1)
    a_ak = _each(lambda m, x: jnp.where(m, x[:c], 0.0), strict, g2)
    b_rk = _each(lambda m, x: jnp.where(m, x[c:], 0.0).astype(BF16), incl, g2)

    t_inv = _each(lambda i, x: i + x, ident, a_ab)
    power = _each(lambda x: _dot(bf(x), stack(bf(x))), a_ab)
    for _ in range(4):
        both = _each(lambda p, t: _dot(bf(jnp.concatenate([p, t], axis=0)), stack(bf(p))),
                     power, t_inv)
        power = [x[:c] for x in both]
        t_inv = _each(lambda t, x: t + x[c:], t_inv, both)
    t_b = _each(lambda t, p: bf(t + _dot(bf(t), stack(bf(p)))), t_inv, power)

    v_stack = _each(stack, v)
    t_ak = _each(lambda t, x: bf(_dot(t, stack(bf(x)))), t_b, a_ak)
    a_p = _each(lambda t, x: bf(_dot(t, stack(x))), t_b, at)
    u0 = _each(lambda t, vs: bf(_dot(t, vs)), t_ak, v_stack)

    r_p = _each(lambda x, w, y: bf(x + _dot(w, stack(y))), rt_f, b_rb, a_p)
    y0 = _each(lambda w, u, w2, vs: _dot(w, stack(u)) + _dot(w2, vs), b_rb, u0, b_rk, v_stack)

    m_bd = _each(lambda x, y, p: bf(jnp.where(bdm, _dot_tn(x, y), 0.0) + jnp.where(eye, p, 0.0)),
                 a_p, bh, p_end)
    n_full = _each(lambda u, y, x, k: jnp.where(bdm, _dot_tn(u, y) + _dot_tn(x, k), 0.0),
                   u0, bh, v, kh)
    n_ls = [sum((x[h * c:(h + 1) * c] for h in range(1, HEADS_PER_GROUP)), x[0:c]) for x in n_full]
    return list(zip(r_p, y0, m_bd, n_ls))


def _wkv_scan_kernel(rf, vf, kkf, kdf, bf, lwf, rb, vb, kkb, kdb, bb, lwb, yf_ref, yb_ref,
                     sf_ref, sb_ref, *, seq_start_blocks):
    step = pl.program_id(1)

    @pl.when(_is_any(step, seq_start_blocks))
    def _():
        sf_ref[...] = jnp.zeros_like(sf_ref)
        sb_ref[...] = jnp.zeros_like(sb_ref)

    g = GROUP_LANES
    bdm = (lax.broadcasted_iota(jnp.int32, (g, g), 0) // HEAD_DIM
           == lax.broadcasted_iota(jnp.int32, (g, g), 1) // HEAD_DIM)
    eye = (lax.broadcasted_iota(jnp.int32, (g, g), 0)
           == lax.broadcasted_iota(jnp.int32, (g, g), 1))
    n_chunks = SCAN_BLOCK // CHUNK

    directions = (((rf, kdf, vf, kkf, bf, lwf), yf_ref, sf_ref, False),
                  ((rb, kdb, vb, kkb, bb, lwb), yb_ref, sb_ref, True))
    ops, masks, flags, rows = [], [], [], []
    for refs, _, _, reverse in directions:
        mask = _scan_masks(reverse)
        for ci in (range(n_chunks - 1, -1, -1) if reverse else range(n_chunks)):
            rw = slice(ci * CHUNK, (ci + 1) * CHUNK)
            rows.append(rw)
            ops.append(tuple(x[rw, :] for x in refs))
            masks.append(mask)
            flags.append(reverse)
    prepared = _scan_prepare(ops, masks, flags, bdm, eye)

    for d, (_, y_ref, s_ref, _) in enumerate(directions):
        state = s_ref[...]
        for i in range(d * n_chunks, (d + 1) * n_chunks):
            r_p, y0, m_bd, n_ls = prepared[i]
            s_b = state.astype(BF16)
            y_ref[rows[i], :] = _dot_nt(r_p, _stack_heads(s_b, bdm)) + y0
            state = _dot(s_b, m_bd) + n_ls
        s_ref[...] = state


def _wkv_scan(r, v, kk, kd0, b0, lw0, kd1, b1, lw1, dims):
    t = dims.tokens
    w = dims.rw_width
    blk = SCAN_BLOCK
    ranges = dims.seq_ranges
    n_groups = w // GROUP_LANES

    def rev_block(s):
        lo, hi = _seg_bounds(s, ranges, blk)
        return lo + hi - 1 - s

    fwd = pl.BlockSpec((blk, GROUP_LANES), lambda g, s: (s, g))
    bwd = pl.BlockSpec((blk, GROUP_LANES), lambda g, s: (rev_block(s), g))
    kernel = functools.partial(_wkv_scan_kernel,
                               seq_start_blocks=tuple(a // blk for a, _ in ranges))
    return pl.pallas_call(
        kernel,
        grid=(n_groups, t // blk),
        in_specs=[fwd] * 6 + [bwd] * 6,
        out_specs=[fwd, bwd],
        out_shape=[jax.ShapeDtypeStruct((t, w), F32)] * 2,
        scratch_shapes=[pltpu.VMEM((HEAD_DIM, GROUP_LANES), F32)] * 2,
        compiler_params=pltpu.CompilerParams(
            dimension_semantics=("arbitrary", "arbitrary"), vmem_limit_bytes=VMEM_LIMIT),
        name="wkv_scan",
    )(r, v, kk, kd0, b0, lw0, r, v, kk, kd1, b1, lw1)


def _rw_post_kernel(yf_ref, yb_ref, bonus_ref, gate_ref, g_ref, b_ref, ones_ref, o_ref):
    y = yf_ref[...] + yb_ref[...]
    mu = _head_sum(y, ones_ref) * (1.0 / HEAD_DIM)
    d = y - mu
    var = _head_sum(d * d, ones_ref) * (1.0 / HEAD_DIM)
    yn = d * lax.rsqrt(var + GN_EPS) * g_ref[...] + b_ref[...]
    o_ref[...] = ((yn + bonus_ref[...]) * gate_ref[...]).astype(o_ref.dtype)


def _rw_post(yf, yb, bonus, gate, g, b, ones, *, tm):
    t, w = yf.shape
    tok = pl.BlockSpec((tm, w), lambda i: (i, 0))
    vec = pl.BlockSpec((1, w), lambda i: (0, 0))
    return pl.pallas_call(
        _rw_post_kernel,
        grid=(t // tm,),
        in_specs=[tok, tok, tok, tok, vec, vec, pl.BlockSpec(ones.shape, lambda i: (0, 0))],
        out_specs=tok,
        out_shape=jax.ShapeDtypeStruct((t, w), BF16),
        compiler_params=pltpu.CompilerParams(
            dimension_semantics=("arbitrary",), vmem_limit_bytes=VMEM_LIMIT),
        name="rw_post",
    )(yf, yb, bonus, gate, g, b, ones)


def _layer_norm_rows(h, g_ref, b_ref):
    mu = jnp.mean(h, axis=-1, keepdims=True)
    d = h - mu
    var = jnp.mean(d * d, axis=-1, keepdims=True)
    return d * lax.rsqrt(var + LN_EPS) * g_ref[...] + b_ref[...]


def _out_proj_kernel(att_ref, rw_ref, w_ref, x_ref, g_ref, b_ref, o_ref, *, n_half, alpha):
    k = pl.program_id(1)

    @pl.when(k == 0)
    def _():
        o_ref[...] = alpha * x_ref[...]

    @pl.when(k < n_half)
    def _():
        o_ref[...] += jnp.dot(att_ref[...], w_ref[...], preferred_element_type=F32)

    @pl.when(k >= n_half)
    def _():
        o_ref[...] += jnp.dot(rw_ref[...], w_ref[...], preferred_element_type=F32)

    @pl.when(k == 2 * n_half - 1)
    def _():
        o_ref[...] = _layer_norm_rows(o_ref[...], g_ref, b_ref)


def _out_proj_ln(att, rw, w, x, g, b, dims, *, tm, tk):
    t, d = x.shape
    n_half = dims.att_width // tk
    vec = pl.BlockSpec((1, d), lambda i, k: (0, 0))
    return pl.pallas_call(
        functools.partial(_out_proj_kernel, n_half=n_half, alpha=dims.alpha),
        grid=(t // tm, 2 * n_half),
        in_specs=[pl.BlockSpec((tm, tk), lambda i, k: (i, jnp.minimum(k, n_half - 1))),
                  pl.BlockSpec((tm, tk), lambda i, k: (i, jnp.maximum(k - n_half, 0))),
                  pl.BlockSpec((tk, d), lambda i, k: (k, 0)),
                  pl.BlockSpec((tm, d), lambda i, k: (i, 0)), vec, vec],
        out_specs=pl.BlockSpec((tm, d), lambda i, k: (i, 0)),
        out_shape=jax.ShapeDtypeStruct((t, d), F32),
        compiler_params=pltpu.CompilerParams(
            dimension_semantics=("arbitrary", "arbitrary"), vmem_limit_bytes=VMEM_LIMIT),
        name="out_proj_ln",
    )(att, rw, w, x, g, b)


def _mlp_kernel(x_ref, w1_ref, w2_ref, g_ref, b_ref, o_ref, xb_ref, *, alpha):
    f = pl.program_id(1)

    @pl.when(f == 0)
    def _():
        x = x_ref[...]
        xb_ref[...] = x.astype(BF16)
        o_ref[...] = alpha * x

    a = jnp.dot(xb_ref[...], w1_ref[...], preferred_element_type=F32)
    a = jnp.square(jnp.maximum(a, 0.0)).astype(BF16)
    o_ref[...] += jnp.dot(a, w2_ref[...], preferred_element_type=F32)

    @pl.when(f == pl.num_programs(1) - 1)
    def _():
        o_ref[...] = _layer_norm_rows(o_ref[...], g_ref, b_ref)


def _mlp_ln(x, w1, w2, g, b, dims, *, tm):
    t, d = x.shape
    nf, _, tf = w1.shape
    vec = pl.BlockSpec((1, d), lambda i, f: (0, 0))
    return pl.pallas_call(
        functools.partial(_mlp_kernel, alpha=dims.alpha),
        grid=(t // tm, nf),
        in_specs=[pl.BlockSpec((tm, d), lambda i, f: (i, 0), pipeline_mode=pl.Buffered(1)),
                  pl.BlockSpec((None, d, tf), lambda i, f: (f, 0, 0)),
                  pl.BlockSpec((tf, d), lambda i, f: (f, 0)), vec, vec],
        out_specs=pl.BlockSpec((tm, d), lambda i, f: (i, 0)),
        out_shape=jax.ShapeDtypeStruct((t, d), F32),
        scratch_shapes=[pltpu.VMEM((tm, d), BF16)],
        compiler_params=pltpu.CompilerParams(
            dimension_semantics=("arbitrary", "arbitrary"), vmem_limit_bytes=VMEM_LIMIT),
        name="mlp_ln",
    )(x, w1, w2, g, b)


def _pad_cols(a, n):
    return jnp.pad(a, [(0, 0)] * (a.ndim - 1) + [(0, n - a.shape[-1])])


def _pad_rows(a, n):
    return jnp.pad(a, [(0, 0)] * (a.ndim - 2) + [(0, n - a.shape[-2]), (0, 0)])


def _rw_columns(a, dims):
    lay, _, _ = dims.rw_layout
    parts = [_pad_cols(a[..., src:src + n], npad) for src, _, n, npad in lay.values()]
    return jnp.concatenate(parts, axis=-1)


def _col_tiles(w, tn):
    k, n = w.shape
    return w.reshape(k, n // tn, tn).transpose(1, 0, 2)


def _layer_params(P, l, dims):
    att_cols = 3 * dims.att_width
    w_in = P["w_in"][l]
    w_rw = _rw_columns(w_in[:, att_cols:], dims)
    if l > 0:
        w_rw = jnp.concatenate([w_rw, _pad_cols(P["w_vres_in"][l - 1], LANES)], axis=1)
    w_rw = _pad_cols(w_rw, _round_up(w_rw.shape[1], PROJ_TILE))
    row = lambda a: a.reshape(1, -1)
    p = dict(
        w_qkv=_col_tiles(w_in[:, :att_cols].astype(BF16), _tile(dims.att_width, PROJ_TILE)),
        w_rw=_col_tiles(w_rw.astype(BF16), PROJ_TILE),
        mu_prev=row(_rw_columns(P["mu_prev"][l], dims)),
        mu_next=row(_rw_columns(P["mu_next"][l], dims)),
        k_k=row(P["k_k"][l]), k_a=row(P["k_a"][l]), r_k=row(P["r_k"][l]),
        w0=P["w0"][l][:, None, :], a0=P["a0"][l][:, None, :],
        wup=_pad_rows(P["w_lora_up"][l], _round_up(dims.decay_lora, LANES)).astype(BF16),
        aup=_pad_rows(P["a_lora_up"][l], _round_up(dims.aaa_lora, LANES)).astype(BF16),
        gup=_pad_rows(P["g_lora_up"][l], _round_up(dims.gate_lora, LANES)).astype(BF16),
        lnx_g=row(P["lnx_g"][l]), lnx_b=row(P["lnx_b"][l]),
        w_out=P["w_out"][l].astype(BF16),
        ln1_g=row(P["ln1_g"][l]), ln1_b=row(P["ln1_b"][l]),
        w_ff1=_col_tiles(P["w_ff1"][l].astype(BF16), _tile(dims.d_ff, MLP_TILE)),
        w_ff2=P["w_ff2"][l].astype(BF16),
        ln2_g=row(P["ln2_g"][l]), ln2_b=row(P["ln2_b"][l]),
    )
    if l > 0:
        p["v0"] = row(P["v0"][l - 1])
        p["vup"] = _pad_rows(P["v_lora_up"][l - 1], LANES).astype(BF16)
    return p


def _tile(n, pref):
    while n % pref:
        pref //= 2
    return pref


def _trunk(x, P, dims):
    t = dims.tokens
    tm = _tile(t, 512)
    rot = _rotary_tables(dims)
    ones = jnp.asarray(np.kron(np.eye(HEADS_PER_GROUP), np.ones((HEAD_DIM, HEAD_DIM))), BF16)
    v_first = None
    for l in range(dims.depth):
        p = _layer_params(P, l, dims)
        qkv = _proj_qkv(x, p["w_qkv"], *rot, tm=tm)
        att = _attention(qkv, dims)
        z = _proj_plain(x, p["w_rw"], tm=tm)
        prep = _rw_prep(z, p, v_first, dims, tm=_tile(t, 128), ones=ones)
        r, v, kk, kd0, b0, lw0, kd1, b1, lw1, bonus, gate = prep[:11]
        if l == 0:
            v_first = prep[11]
        yf, yb = _wkv_scan(r, v, kk, kd0, b0, lw0, kd1, b1, lw1, dims)
        rw = _rw_post(yf, yb, bonus, gate, p["lnx_g"], p["lnx_b"], ones, tm=tm)
        x = _out_proj_ln(att, rw, p["w_out"], x, p["ln1_g"], p["ln1_b"], dims,
                         tm=tm, tk=_tile(dims.att_width, 512))
        x = _mlp_ln(x, p["w_ff1"], p["w_ff2"], p["ln2_g"], p["ln2_b"], dims, tm=tm)
    return x


def _lora_dim(scale, power, d_model):
    return max(32, int(round(scale * d_model ** power / 32)) * 32)


def _make_dims(d_model, d_ff, depth, seqs):
    return Dims(d_model=d_model, d_ff=d_ff, depth=depth, seqs=seqs,
                decay_lora=_lora_dim(1.8, 0.5, d_model), aaa_lora=_lora_dim(1.8, 0.5, d_model),
                mv_lora=_lora_dim(1.3, 0.5, d_model), gate_lora=_lora_dim(0.6, 0.8, d_model))


def kernel(x_prompt, x_sample, w_in, w_vres_in, mu_prev, mu_next, w0, w_lora_up, a0, a_lora_up, v0, v_lora_up, g_lora_up, k_k, k_a, r_k, lnx_g, lnx_b, w_out, ln1_g, ln1_b, w_ff1, w_ff2, ln2_g, ln2_b):
    P = dict(w_in=w_in, w_vres_in=w_vres_in, mu_prev=mu_prev, mu_next=mu_next, w0=w0,
             w_lora_up=w_lora_up, a0=a0, a_lora_up=a_lora_up, v0=v0, v_lora_up=v_lora_up,
             g_lora_up=g_lora_up, k_k=k_k, k_a=k_a, r_k=r_k, lnx_g=lnx_g, lnx_b=lnx_b,
             w_out=w_out, ln1_g=ln1_g, ln1_b=ln1_b, w_ff1=w_ff1, w_ff2=w_ff2,
             ln2_g=ln2_g, ln2_b=ln2_b)
    d_model = x_prompt.shape[-1]
    dims = _make_dims(d_model, w_ff1.shape[-1], w_in.shape[0],
                      (tuple(x_prompt.shape[:2]), tuple(x_sample.shape[:2])))
    n_prompt = x_prompt.shape[0] * x_prompt.shape[1]
    x = jnp.concatenate([x_prompt.reshape(-1, d_model), x_sample.reshape(-1, d_model)], axis=0)
    y = _trunk(x, P, dims)
    return y[:n_prompt].reshape(x_prompt.shape), y[n_prompt:].reshape(x_sample.shape)
```
